```python
import math
import jax, jax.numpy as jnp
from jax import lax
import numpy as np

D_MODEL = 2048
BATCH = 1
SEQ = 8192
DEPTH = 4

HEAD_DIM = 128
D_MIX = D_MODEL
A_VDIM = 2 * HEAD_DIM
A_HEADS = (D_MIX // 2) // A_VDIM
B_HEADS = (D_MIX // 4) // HEAD_DIM
C_HEADS = (D_MIX // 4) // HEAD_DIM
IDX_HEADS = 16
IDX_DIM = 128
D_FF = 4 * D_MODEL
N_BUCKETS = 32
MAX_DISTANCE = 128
N_BIAS_HEADS = A_HEADS + C_HEADS
TOPK_CAP = 256
Q_BLOCK = 128
ALPHA = (2 * DEPTH) ** 0.25
INIT_BETA = (8 * DEPTH) ** -0.25
LN_EPS = 1e-5
RMS_EPS = 1e-5
NEG_INF = -1e30

SPLIT_SIZES = [
    A_HEADS * 2 * HEAD_DIM,
    A_HEADS * 2 * HEAD_DIM,
    A_HEADS * A_VDIM,
    B_HEADS * HEAD_DIM,
    B_HEADS * HEAD_DIM,
    B_HEADS * HEAD_DIM,
    C_HEADS * HEAD_DIM,
    C_HEADS * HEAD_DIM,
    C_HEADS * HEAD_DIM,
    IDX_HEADS * IDX_DIM,
    IDX_DIM,
    IDX_HEADS,
]
N_IN = sum(SPLIT_SIZES)
SPLIT_POINTS = [int(v) for v in np.cumsum(SPLIT_SIZES)[:-1]]

kernel_name = 'hybrid_diff_stickbreak_dsa_deepnorm'


def layer_norm(x, g, b):
    xf = x.astype(jnp.float32)
    mu = jnp.mean(xf, -1, keepdims=True)
    var = jnp.mean(jnp.square(xf - mu), -1, keepdims=True)
    y = (xf - mu) * lax.rsqrt(var + LN_EPS) * g.astype(jnp.float32) + b.astype(jnp.float32)
    return y.astype(x.dtype)


def rms_norm(x, g):
    xf = x.astype(jnp.float32)
    y = xf * lax.rsqrt(jnp.mean(xf * xf, -1, keepdims=True) + RMS_EPS) * g.astype(jnp.float32)
    return y.astype(x.dtype)


def rel_bucket(dist):
    n = jnp.maximum(dist, 0)
    max_exact = N_BUCKETS // 2
    nf = jnp.maximum(n, 1).astype(jnp.float32)
    large = max_exact + (jnp.log(nf / max_exact) / math.log(MAX_DISTANCE / max_exact)
                         * (N_BUCKETS - max_exact)).astype(jnp.int32)
    large = jnp.minimum(large, N_BUCKETS - 1)
    return jnp.where(n < max_exact, n, large)


def to_blocks(t):
    b, s = t.shape[0], t.shape[1]
    t = t.reshape((b, s // Q_BLOCK, Q_BLOCK) + t.shape[2:])
    return jnp.moveaxis(t, 1, 0)


def from_blocks(t):
    t = jnp.moveaxis(t, 0, 1)
    return t.reshape((t.shape[0], t.shape[1] * t.shape[2]) + t.shape[3:])


def diff_attention(q, k, v, lam, bias_table):
    s_len = k.shape[1]
    k_pos = jnp.arange(s_len)
    scale = HEAD_DIM ** -0.5
    v32 = v.astype(jnp.float32)
    table = bias_table.astype(jnp.float32)

    def block(args):
        qb, start = args
        q_pos = start + jnp.arange(Q_BLOCK)
        dist = q_pos[:, None] - k_pos[None, :]
        bias = jnp.moveaxis(table[rel_bucket(dist)], -1, 0)
        s = jnp.einsum('bqhmd,bkhmd->bhmqk', qb, k).astype(jnp.float32) * scale
        s = s + bias[None, :, None]
        s = jnp.where((dist >= 0)[None, None, None], s, NEG_INF)
        p = jax.nn.softmax(s, axis=-1)
        p_diff = p[:, :, 0] - lam * p[:, :, 1]
        return jnp.einsum('bhqk,bkhe->bqhe', p_diff, v32).astype(v.dtype)

    starts = jnp.arange(s_len // Q_BLOCK) * Q_BLOCK
    return from_blocks(lax.map(block, (to_blocks(q), starts)))


def stick_breaking_attention(q, k, v):
    s_len = k.shape[1]
    k_pos = jnp.arange(s_len)
    scale = HEAD_DIM ** -0.5
    v32 = v.astype(jnp.float32)

    def block(args):
        qb, start = args
        q_pos = start + jnp.arange(Q_BLOCK)
        strict = (q_pos[:, None] > k_pos[None, :])[None, None]
        z = jnp.einsum('bqhd,bkhd->bhqk', qb, k).astype(jnp.float32) * scale
        log_beta = jax.nn.log_sigmoid(z)
        log_keep = jnp.where(strict, jax.nn.log_sigmoid(-z), 0.0)
        log_after = lax.cumsum(log_keep, axis=3, reverse=True) - log_keep
        a = jnp.where(strict, jnp.exp(log_beta + log_after), 0.0)
        return jnp.einsum('bhqk,bkhd->bqhd', a, v32).astype(v.dtype)

    starts = jnp.arange(s_len // Q_BLOCK) * Q_BLOCK
    return from_blocks(lax.map(block, (to_blocks(q), starts)))


def dsa_attention(q, k, v, q_idx, k_idx, w_idx, bias_table, top_k):
    s_len = k.shape[1]
    k_pos = jnp.arange(s_len)
    scale = HEAD_DIM ** -0.5
    table = bias_table.astype(jnp.float32)
    gather = jax.vmap(lambda arr, ids: arr[ids])

    def block(args):
        qb, qib, wb, start = args
        q_pos = start + jnp.arange(Q_BLOCK)
        causal = q_pos[:, None] >= k_pos[None, :]
        idx_logits = jnp.einsum('bqhd,bkd->bqhk', qib, k_idx).astype(jnp.float32)
        score = jnp.einsum('bqhk,bqh->bqk', jax.nn.relu(idx_logits), wb.astype(jnp.float32))
        score = jnp.where(causal[None], score, -jnp.inf)
        _, sel = lax.top_k(score, top_k)
        k_sel = gather(k, sel)
        v_sel = gather(v, sel).astype(jnp.float32)
        dist = q_pos[None, :, None] - sel
        bias = jnp.transpose(table[rel_bucket(dist)], (0, 3, 1, 2))
        s = jnp.einsum('bqhd,bqkhd->bhqk', qb, k_sel).astype(jnp.float32) * scale + bias
        s = jnp.where((dist >= 0)[:, None], s, NEG_INF)
        p = jax.nn.softmax(s, axis=-1)
        return jnp.einsum('bhqk,bqkhd->bqhd', p, v_sel).astype(v.dtype)

    starts = jnp.arange(s_len // Q_BLOCK) * Q_BLOCK
    return from_blocks(lax.map(block, (to_blocks(q), to_blocks(q_idx), to_blocks(w_idx), starts)))


def hybrid_mixer(x, w_in, w_out, lam_vecs, lam_init, a_g, b_g, c_g, ikn_g, ikn_b, rel_bias):
    b, s, _ = x.shape
    h = x @ w_in
    aq, ak, av, bq, bk, bv, cq, ck, cv, iq, ik, iw = jnp.split(h, SPLIT_POINTS, axis=-1)

    lv = lam_vecs.astype(jnp.float32)
    lam = jnp.exp(jnp.sum(lv[0] * lv[1])) - jnp.exp(jnp.sum(lv[2] * lv[3])) + lam_init
    a_out = diff_attention(aq.reshape(b, s, A_HEADS, 2, HEAD_DIM),
                           ak.reshape(b, s, A_HEADS, 2, HEAD_DIM),
                           av.reshape(b, s, A_HEADS, A_VDIM),
                           lam, rel_bias[:, :A_HEADS])
    a_out = (rms_norm(a_out, a_g) * (1.0 - lam_init)).reshape(b, s, A_HEADS * A_VDIM)

    b_out = stick_breaking_attention(bq.reshape(b, s, B_HEADS, HEAD_DIM),
                                     bk.reshape(b, s, B_HEADS, HEAD_DIM),
                                     bv.reshape(b, s, B_HEADS, HEAD_DIM))
    b_out = rms_norm(b_out.reshape(b, s, B_HEADS * HEAD_DIM), b_g)

    top_k = min(TOPK_CAP, s // 4)
    k_idx = layer_norm(ik, ikn_g, ikn_b)
    w_idx = iw * (IDX_HEADS ** -0.5 * IDX_DIM ** -0.5)
    c_out = dsa_attention(cq.reshape(b, s, C_HEADS, HEAD_DIM),
                          ck.reshape(b, s, C_HEADS, HEAD_DIM),
                          cv.reshape(b, s, C_HEADS, HEAD_DIM),
                          iq.reshape(b, s, IDX_HEADS, IDX_DIM), k_idx, w_idx,
                          rel_bias[:, A_HEADS:], top_k)
    c_out = rms_norm(c_out.reshape(b, s, C_HEADS * HEAD_DIM), c_g)

    mixed = jnp.concatenate([a_out, b_out, c_out], axis=-1)
    return mixed @ w_out


def setup_inputs(seed: int = 0) -> dict:
    key = jax.random.key(seed)
    ks = jax.random.split(key, 16)
    f32 = jnp.float32
    x = jax.random.normal(ks[0], (BATCH, SEQ, D_MODEL), f32)
    w_in = jax.random.normal(ks[1], (DEPTH, D_MODEL, N_IN), f32) * D_MODEL ** -0.5
    w_out = jax.random.normal(ks[2], (DEPTH, D_MIX, D_MODEL), f32) * (D_MIX ** -0.5 * INIT_BETA)
    lam_vecs = jax.random.normal(ks[3], (DEPTH, 4, HEAD_DIM), f32) * 0.1
    a_norm_g = 1.0 + 0.02 * jax.random.normal(ks[4], (DEPTH, A_VDIM), f32)
    b_norm_g = 1.0 + 0.02 * jax.random.normal(ks[5], (DEPTH, B_HEADS * HEAD_DIM), f32)
    c_norm_g = 1.0 + 0.02 * jax.random.normal(ks[6], (DEPTH, C_HEADS * HEAD_DIM), f32)
    idx_k_norm_g = 1.0 + 0.02 * jax.random.normal(ks[7], (DEPTH, IDX_DIM), f32)
    idx_k_norm_b = 0.02 * jax.random.normal(ks[8], (DEPTH, IDX_DIM), f32)
    ln1_g = 1.0 + 0.02 * jax.random.normal(ks[9], (DEPTH, D_MODEL), f32)
    ln1_b = 0.02 * jax.random.normal(ks[10], (DEPTH, D_MODEL), f32)
    w_up = jax.random.normal(ks[11], (DEPTH, D_MODEL, D_FF), f32) * D_MODEL ** -0.5
    w_down = jax.random.normal(ks[12], (DEPTH, D_FF, D_MODEL), f32) * (D_FF ** -0.5 * INIT_BETA)
    ln2_g = 1.0 + 0.02 * jax.random.normal(ks[13], (DEPTH, D_MODEL), f32)
    ln2_b = 0.02 * jax.random.normal(ks[14], (DEPTH, D_MODEL), f32)
    rel_bias = 0.2 * jax.random.normal(ks[15], (N_BUCKETS, N_BIAS_HEADS), f32)
    return {'x': x, 'w_in': w_in, 'w_out': w_out, 'lam_vecs': lam_vecs,
            'a_norm_g': a_norm_g, 'b_norm_g': b_norm_g, 'c_norm_g': c_norm_g,
            'idx_k_norm_g': idx_k_norm_g, 'idx_k_norm_b': idx_k_norm_b,
            'ln1_g': ln1_g, 'ln1_b': ln1_b, 'w_up': w_up, 'w_down': w_down,
            'ln2_g': ln2_g, 'ln2_b': ln2_b, 'rel_bias': rel_bias}


def reference(x, w_in, w_out, lam_vecs, a_norm_g, b_norm_g, c_norm_g, idx_k_norm_g, idx_k_norm_b,
              ln1_g, ln1_b, w_up, w_down, ln2_g, ln2_b, rel_bias):
    for l in range(DEPTH):
        lam_init = 0.8 - 0.6 * math.exp(-0.3 * l)
        m = hybrid_mixer(x, w_in[l], w_out[l], lam_vecs[l], lam_init, a_norm_g[l], b_norm_g[l],
                         c_norm_g[l], idx_k_norm_g[l], idx_k_norm_b[l], rel_bias)
        x = layer_norm(ALPHA * x + m, ln1_g[l], ln1_b[l])
        hdn = jnp.square(jax.nn.relu(x @ w_up[l]))
        x = layer_norm(ALPHA * x + hdn @ w_down[l], ln2_g[l], ln2_b[l])
    return x
```

```python
import functools
import math

import jax
import jax.numpy as jnp
from jax import lax
from jax.experimental import pallas as pl
from jax.experimental.pallas import tpu as pltpu

F32 = jnp.float32
BF16 = jnp.bfloat16

D_MODEL = 2048
HEAD_DIM = 128
A_HEADS = 4
A_VDIM = 2 * HEAD_DIM
B_HEADS = 4
C_HEADS = 4
IDX_HEADS = 16
IDX_DIM = 128
D_FF = 4 * D_MODEL
N_BUCKETS = 32
MAX_DISTANCE = 128
TOPK_CAP = 256
LN_EPS = 1e-5
RMS_EPS = 1e-5
NEG_INF = -1e30

A_Q_OFF = 0
A_K_OFF = A_Q_OFF + A_HEADS * 2 * HEAD_DIM
A_V_OFF = A_K_OFF + A_HEADS * 2 * HEAD_DIM
B_Q_OFF = A_V_OFF + A_HEADS * A_VDIM
B_K_OFF = B_Q_OFF + B_HEADS * HEAD_DIM
B_V_OFF = B_K_OFF + B_HEADS * HEAD_DIM
C_Q_OFF = B_V_OFF + B_HEADS * HEAD_DIM
C_K_OFF = C_Q_OFF + C_HEADS * HEAD_DIM
C_V_OFF = C_K_OFF + C_HEADS * HEAD_DIM
I_Q_OFF = C_V_OFF + C_HEADS * HEAD_DIM
I_K_OFF = I_Q_OFF + IDX_HEADS * IDX_DIM
I_W_OFF = I_K_OFF + IDX_DIM
N_IN = I_W_OFF + IDX_HEADS
A_WIDTH = A_HEADS * A_VDIM
B_WIDTH = B_HEADS * HEAD_DIM
C_WIDTH = C_HEADS * HEAD_DIM

LANES = 128
ATT_TILE = 256
TAIL_WIDTH = 2 * LANES
VMEM_LIMIT = 56 * 1024 * 1024

MASK_NEG = -1e30
M_INIT = -1e29
INT_MIN = -(2 ** 31)

NT_DIMS = (((1,), (1,)), ((), ()))


def _params(n_axes):
    return pltpu.CompilerParams(dimension_semantics=("arbitrary",) * n_axes,
                                vmem_limit_bytes=VMEM_LIMIT)


def _resident(block_shape, index_map):
    return pl.BlockSpec(block_shape, index_map, pipeline_mode=pl.Buffered(1))


def _layer_norm(y, g, b):
    mu = jnp.mean(y, axis=-1, keepdims=True)
    d = y - mu
    var = jnp.mean(d * d, axis=-1, keepdims=True)
    return d * lax.rsqrt(var + LN_EPS) * g + b


def _rms_norm(y, g):
    return y * lax.rsqrt(jnp.mean(y * y, axis=-1, keepdims=True) + RMS_EPS) * g


def _proj_kernel(x_ref, w_ref, o_ref, *, relu2):
    y = jnp.dot(x_ref[...], w_ref[...], preferred_element_type=F32)
    if relu2:
        y = jnp.square(jnp.maximum(y, 0.0))
    o_ref[...] = y.astype(o_ref.dtype)


def _proj(x16, w16, *, tm, tn, relu2=False):
    m, k = x16.shape
    n = w16.shape[1]
    return pl.pallas_call(
        functools.partial(_proj_kernel, relu2=relu2),
        grid=(n // tn, m // tm),
        in_specs=[pl.BlockSpec((tm, k), lambda j, i: (i, 0)),
                  pl.BlockSpec((k, tn), lambda j, i: (0, j))],
        out_specs=pl.BlockSpec((tm, tn), lambda j, i: (i, j)),
        out_shape=jax.ShapeDtypeStruct((m, n), BF16),
        compiler_params=_params(2),
        name="proj_relu2" if relu2 else "proj",
    )(x16, w16)


def _tail_kernel(x_ref, w_ref, g_ref, b_ref, kidx_ref, widx_ref):
    y = jnp.dot(x_ref[...], w_ref[...], preferred_element_type=F32)
    kidx_ref[...] = _layer_norm(y[:, :IDX_DIM], g_ref[...], b_ref[...]).astype(BF16)
    widx_ref[...] = y[:, IDX_DIM:] * (IDX_HEADS ** -0.5 * IDX_DIM ** -0.5)


def _tail_proj(x16, w_tail16, g, b, *, tm):
    m, k = x16.shape
    return pl.pallas_call(
        _tail_kernel,
        grid=(m // tm,),
        in_specs=[pl.BlockSpec((tm, k), lambda i: (i, 0)),
                  _resident((k, TAIL_WIDTH), lambda i: (0, 0)),
                  _resident((1, IDX_DIM), lambda i: (0, 0)),
                  _resident((1, IDX_DIM), lambda i: (0, 0))],
        out_specs=[pl.BlockSpec((tm, IDX_DIM), lambda i: (i, 0)),
                   pl.BlockSpec((tm, TAIL_WIDTH - IDX_DIM), lambda i: (i, 0))],
        out_shape=[jax.ShapeDtypeStruct((m, IDX_DIM), BF16),
                   jax.ShapeDtypeStruct((m, TAIL_WIDTH - IDX_DIM), F32)],
        compiler_params=_params(1),
        name="tail_proj",
    )(x16, w_tail16, g, b)


def _residual_ln(x, acc, alpha, g, b, o32_ref, o16_ref):
    y = _layer_norm(alpha * x + acc, g, b)
    o32_ref[...] = y
    o16_ref[...] = y.astype(BF16)


def _out_ln_kernel(a_ref, b_ref, c_ref, gb_ref, gc_ref, w_ref, x_ref, lg_ref, lb_ref,
                   o32_ref, o16_ref, *, alpha):
    bn = _rms_norm(b_ref[...], gb_ref[...]).astype(BF16)
    cn = _rms_norm(c_ref[...], gc_ref[...]).astype(BF16)
    acc = jnp.dot(a_ref[...], w_ref[:A_WIDTH, :], preferred_element_type=F32)
    acc += jnp.dot(bn, w_ref[A_WIDTH:A_WIDTH + B_WIDTH, :], preferred_element_type=F32)
    acc += jnp.dot(cn, w_ref[A_WIDTH + B_WIDTH:, :], preferred_element_type=F32)
    _residual_ln(x_ref[...], acc, alpha, lg_ref[...], lb_ref[...], o32_ref, o16_ref)


def _out_proj_ln(a16, b_raw, c_raw, gb, gc, w16, x32, lg, lb, *, alpha, tm):
    m, d = x32.shape
    row = lambda i: (i, 0)
    fixed = lambda i: (0, 0)
    return pl.pallas_call(
        functools.partial(_out_ln_kernel, alpha=alpha),
        grid=(m // tm,),
        in_specs=[pl.BlockSpec((tm, A_WIDTH), row),
                  pl.BlockSpec((tm, B_WIDTH), row),
                  pl.BlockSpec((tm, C_WIDTH), row),
                  _resident((1, B_WIDTH), fixed),
                  _resident((1, C_WIDTH), fixed),
                  _resident(w16.shape, fixed),
                  pl.BlockSpec((tm, d), row),
                  _resident((1, d), fixed),
                  _resident((1, d), fixed)],
        out_specs=[pl.BlockSpec((tm, d), row), pl.BlockSpec((tm, d), row)],
        out_shape=[jax.ShapeDtypeStruct((m, d), F32), jax.ShapeDtypeStruct((m, d), BF16)],
        compiler_params=_params(1),
        name="out_proj_ln",
    )(a16, b_raw, c_raw, gb, gc, w16, x32, lg, lb)


def _down_ln_kernel(h_ref, w_ref, x_ref, lg_ref, lb_ref, o32_ref, o16_ref, acc_ref, *, alpha):
    kk = pl.program_id(1)

    @pl.when(kk == 0)
    def _():
        acc_ref[...] = jnp.zeros(acc_ref.shape, F32)

    acc_ref[...] += jnp.dot(h_ref[...], w_ref[...], preferred_element_type=F32)

    @pl.when(kk == pl.num_programs(1) - 1)
    def _():
        _residual_ln(x_ref[...], acc_ref[...], alpha, lg_ref[...], lb_ref[...], o32_ref, o16_ref)


def _down_proj_ln(h16, w16, x32, lg, lb, *, alpha, tm, tk):
    m, d = x32.shape
    k = h16.shape[1]
    return pl.pallas_call(
        functools.partial(_down_ln_kernel, alpha=alpha),
        grid=(m // tm, k // tk),
        in_specs=[pl.BlockSpec((tm, tk), lambda i, kk: (i, kk)),
                  pl.BlockSpec((tk, d), lambda i, kk: (kk, 0)),
                  pl.BlockSpec((tm, d), lambda i, kk: (i, 0)),
                  _resident((1, d), lambda i, kk: (0, 0)),
                  _resident((1, d), lambda i, kk: (0, 0))],
        out_specs=[pl.BlockSpec((tm, d), lambda i, kk: (i, 0)),
                   pl.BlockSpec((tm, d), lambda i, kk: (i, 0))],
        out_shape=[jax.ShapeDtypeStruct((m, d), F32), jax.ShapeDtypeStruct((m, d), BF16)],
        scratch_shapes=[pltpu.VMEM((tm, d), F32)],
        compiler_params=_params(2),
        name="down_proj_ln",
    )(h16, w16, x32, lg, lb)


def _rel_bucket(dist):
    n = jnp.maximum(dist, 0)
    max_exact = N_BUCKETS // 2
    nf = jnp.maximum(n, 1).astype(F32)
    large = max_exact + (jnp.log(nf / max_exact) / math.log(MAX_DISTANCE / max_exact)
                         * (N_BUCKETS - max_exact)).astype(jnp.int32)
    large = jnp.minimum(large, N_BUCKETS - 1)
    return jnp.where(n < max_exact, n, large)


def _bias_tiles(table, tile):
    assert tile >= MAX_DISTANCE
    table = table.astype(F32)
    qi = jnp.arange(tile)[:, None]
    kj = jnp.arange(tile)[None, :]
    far = table[_rel_bucket(jnp.asarray(2 * tile))]
    d_diag = qi - kj
    diag = jnp.where((d_diag >= 0)[..., None], table[_rel_bucket(d_diag)] - far, NEG_INF)
    prev = table[_rel_bucket(tile + d_diag)] - far
    return jnp.transpose(jnp.stack([diag, prev], axis=0), (3, 0, 1, 2))


def _scaled_q(q_ref):
    return (q_ref[...].astype(F32) * (HEAD_DIM ** -0.5)).astype(BF16)


def _softmax_step(s, v, m_ref, l_ref, acc_ref, mi, acc_cols):
    m_old = m_ref[mi]
    m_new = jnp.maximum(m_old, jnp.max(s, axis=-1, keepdims=True))
    p = jnp.exp(s - m_new)
    alpha = jnp.exp(m_old - m_new)
    l_ref[mi] = alpha * l_ref[mi] + jnp.sum(p, axis=-1, keepdims=True)
    acc_ref[acc_cols] = alpha * acc_ref[acc_cols] + jnp.dot(p.astype(BF16), v,
                                                           preferred_element_type=F32)
    m_ref[mi] = m_new


def _diff_attn_kernel(lamv_ref, g_ref, bias_ref, q_ref, k_ref, v_ref, o_ref,
                      m_ref, l_ref, acc_ref, *, tile, lam_init):
    i = pl.program_id(1)
    q = _scaled_q(q_ref)
    m_ref[...] = jnp.full(m_ref.shape, M_INIT, F32)
    l_ref[...] = jnp.zeros(l_ref.shape, F32)
    acc_ref[...] = jnp.zeros(acc_ref.shape, F32)

    def step(j, bias):
        row0 = pl.multiple_of(j * tile, tile)
        kj = k_ref[pl.ds(row0, tile), :]
        vj = v_ref[pl.ds(row0, tile), :]
        for mm in range(2):
            cols = slice(mm * HEAD_DIM, (mm + 1) * HEAD_DIM)
            s = lax.dot_general(q[:, cols], kj[:, cols], NT_DIMS, preferred_element_type=F32)
            if bias is not None:
                s = s + bias
            _softmax_step(s, vj, m_ref, l_ref, acc_ref, mm, mm)

    def far_body(j, carry):
        step(j, None)
        return carry

    lax.fori_loop(0, jnp.maximum(i - 1, 0), far_body, 0)

    @pl.when(i >= 1)
    def _():
        step(i - 1, bias_ref[0, 1])

    step(i, bias_ref[0, 0])

    lv = lamv_ref[...]
    lam = (jnp.exp(jnp.sum(lv[0:1] * lv[1:2], axis=-1, keepdims=True))
           - jnp.exp(jnp.sum(lv[2:3] * lv[3:4], axis=-1, keepdims=True)) + lam_init)
    o = acc_ref[0] / l_ref[0] - lam * (acc_ref[1] / l_ref[1])
    o_ref[...] = (_rms_norm(o, g_ref[...]) * (1.0 - lam_init)).astype(o_ref.dtype)


def _diff_attention(h16, lam_vecs, g, bias, *, lam_init, tile):
    s = h16.shape[0]
    qb, kb, vb = A_Q_OFF // A_VDIM, A_K_OFF // A_VDIM, A_V_OFF // A_VDIM
    return pl.pallas_call(
        functools.partial(_diff_attn_kernel, tile=tile, lam_init=lam_init),
        grid=(A_HEADS, s // tile),
        in_specs=[_resident(lam_vecs.shape, lambda h, i: (0, 0)),
                  _resident((1, A_VDIM), lambda h, i: (0, 0)),
                  _resident((1, 2, tile, tile), lambda h, i: (h, 0, 0, 0)),
                  pl.BlockSpec((tile, 2 * HEAD_DIM), lambda h, i: (i, qb + h)),
                  _resident((s, 2 * HEAD_DIM), lambda h, i: (0, kb + h)),
                  _resident((s, A_VDIM), lambda h, i: (0, vb + h))],
        out_specs=pl.BlockSpec((tile, A_VDIM), lambda h, i: (i, h)),
        out_shape=jax.ShapeDtypeStruct((s, A_WIDTH), BF16),
        scratch_shapes=[pltpu.VMEM((2, tile, 1), F32), pltpu.VMEM((2, tile, 1), F32),
                        pltpu.VMEM((2, tile, A_VDIM), F32)],
        compiler_params=_params(2),
        name="diff_attention",
    )(lam_vecs, g, bias, h16, h16, h16)


def _stick_kernel(u_ref, q_ref, k_ref, v_ref, o_ref, c_ref, acc_ref, *, tile):
    i = pl.program_id(1)
    q = _scaled_q(q_ref)
    c_ref[...] = jnp.zeros(c_ref.shape, F32)
    acc_ref[...] = jnp.zeros(acc_ref.shape, F32)

    def step(j, diag):
        row0 = pl.multiple_of(j * tile, tile)
        kj = k_ref[pl.ds(row0, tile), :]
        vj = v_ref[pl.ds(row0, tile), :]
        z = lax.dot_general(q, kj, NT_DIMS, preferred_element_type=F32)
        sp = jnp.log1p(jnp.exp(-jnp.abs(z)))
        log_beta = jnp.minimum(z, 0.0) - sp
        log_keep = -(jnp.maximum(z, 0.0) + sp)
        if diag:
            strict = (lax.broadcasted_iota(jnp.int32, z.shape, 0)
                      > lax.broadcasted_iota(jnp.int32, z.shape, 1))
            log_keep = jnp.where(strict, log_keep, 0.0)
        hi = log_keep.astype(BF16)
        lo = (log_keep - hi.astype(F32)).astype(BF16)
        u = u_ref[...]
        cum = (jnp.dot(hi, u, preferred_element_type=F32)
               + jnp.dot(lo, u, preferred_element_type=F32))
        a = jnp.exp(log_beta + cum + c_ref[...])
        if diag:
            a = jnp.where(strict, a, 0.0)
        acc_ref[...] += jnp.dot(a.astype(BF16), vj, preferred_element_type=F32)
        c_ref[...] += cum[:, 0:1] + log_keep[:, 0:1]

    step(i, True)

    def body(t, carry):
        step(i - 1 - t, False)
        return carry

    lax.fori_loop(0, i, body, 0)
    o_ref[...] = acc_ref[...]


def _stick_breaking_attention(h16, *, tile):
    s = h16.shape[0]
    qb, kb, vb = B_Q_OFF // HEAD_DIM, B_K_OFF // HEAD_DIM, B_V_OFF // HEAD_DIM
    idx = jnp.arange(tile)
    upper = (idx[:, None] > idx[None, :]).astype(BF16)
    return pl.pallas_call(
        functools.partial(_stick_kernel, tile=tile),
        grid=(B_HEADS, s // tile),
        in_specs=[_resident((tile, tile), lambda h, i: (0, 0)),
                  pl.BlockSpec((tile, HEAD_DIM), lambda h, i: (i, qb + h)),
                  _resident((s, HEAD_DIM), lambda h, i: (0, kb + h)),
                  _resident((s, HEAD_DIM), lambda h, i: (0, vb + h))],
        out_specs=pl.BlockSpec((tile, HEAD_DIM), lambda h, i: (i, h)),
        out_shape=jax.ShapeDtypeStruct((s, B_WIDTH), F32),
        scratch_shapes=[pltpu.VMEM((tile, 1), F32), pltpu.VMEM((tile, HEAD_DIM), F32)],
        compiler_params=_params(2),
        name="stick_breaking_attention",
    )(upper, h16, h16, h16)


def _dsa_select_kernel(iq_ref, w_ref, kidx_ref, o_ref, key_ref, wb_ref, bound_ref,
                       *, tile, top_k, seq):
    i = pl.program_id(0)
    n_tiles = seq // tile
    kf = float(top_k)
    row = lax.broadcasted_iota(jnp.int32, (tile, tile), 0)
    col = lax.broadcasted_iota(jnp.int32, (tile, tile), 1)
    causal = row >= col

    w = w_ref[...]
    for h in range(IDX_HEADS):
        wb_ref[h] = jnp.broadcast_to(w[:, h:h + 1], (tile, LANES))

    def score_tile(c, carry):
        row0 = pl.multiple_of(c * tile, tile)
        kc = kidx_ref[pl.ds(row0, tile), :]
        acc = jnp.zeros((tile, tile), F32)
        for h in range(IDX_HEADS):
            lg = lax.dot_general(iq_ref[:, h * IDX_DIM:(h + 1) * IDX_DIM], kc, NT_DIMS,
                                 preferred_element_type=F32)
            wb = wb_ref[h]
            acc = acc + jnp.maximum(lg, 0.0) * jnp.concatenate([wb] * (tile // LANES), axis=1)
        bits = lax.bitcast_convert_type(acc, jnp.int32)
        key = bits ^ ((bits >> 31) & 0x7FFFFFFF)
        key_ref[c] = jnp.where(bits == INT_MIN, 0, key)
        return carry

    lax.fori_loop(0, i + 1, score_tile, 0)
    key_ref[i] = jnp.where(causal, key_ref[i], INT_MIN)

    def fold(x):
        out = x[:, :LANES]
        for t in range(1, tile // LANES):
            out = out + x[:, t * LANES:(t + 1) * LANES]
        return out

    def count(pred):
        def body(c, cnt):
            return cnt + fold(jnp.where(pred(key_ref[c], c), 1.0, 0.0))
        cnt = lax.fori_loop(0, i + 1, body, jnp.zeros((tile, LANES), F32))
        return jnp.sum(cnt, axis=-1, keepdims=True)

    zero = jnp.zeros((tile, 1), jnp.int32)
    thr = jnp.where(count(lambda k, c: k >= zero) >= kf, zero, jnp.full((tile, 1), INT_MIN, jnp.int32))

    def bit_body(t, thr):
        cand = thr + jnp.left_shift(jnp.int32(1), 30 - t)
        return jnp.where(count(lambda k, c: k >= cand) >= kf, cand, thr)

    thr = lax.fori_loop(0, 31, bit_body, thr)

    n_gt = count(lambda k, c: k > thr)
    n_eq = count(lambda k, c: k == thr)
    need = kf - n_gt
    bound_ref[...] = jnp.full((tile, 1), seq, jnp.int32)

    @pl.when(jnp.max(jnp.where((n_eq > need) & (thr > INT_MIN), 1.0, 0.0)) > 0.0)
    def _():
        def idx_body(t, pos):
            cand = pos + jnp.left_shift(jnp.int32(1), (seq.bit_length() - 1) - t)
            n = count(lambda k, c: (k == thr) & ((c * tile + col) < cand))
            return jnp.where(n < need, cand, pos)
        pos = lax.fori_loop(0, seq.bit_length(), idx_body, jnp.zeros((tile, 1), jnp.int32))
        bound_ref[...] = pos + 1

    bound = bound_ref[...]

    def selected(c):
        k = key_ref[c]
        return (k > thr) | ((k == thr) & ((c * tile + col) < bound))

    def emit(c, carry):
        o_ref[0, c] = jnp.where(selected(c), 0.0, MASK_NEG).astype(o_ref.dtype)
        return carry

    lax.fori_loop(0, i, emit, 0)
    o_ref[0, i] = jnp.where(selected(i) & causal, 0.0, MASK_NEG).astype(o_ref.dtype)

    def fill(c, carry):
        o_ref[0, c] = jnp.full((tile, tile), MASK_NEG, o_ref.dtype)
        return carry

    lax.fori_loop(i + 1, n_tiles, fill, 0)


def _dsa_select(h16, w_idx, k_idx, *, tile, top_k):
    s = h16.shape[0]
    n_tiles = s // tile
    iq_width = IDX_HEADS * IDX_DIM
    return pl.pallas_call(
        functools.partial(_dsa_select_kernel, tile=tile, top_k=top_k, seq=s),
        grid=(n_tiles,),
        in_specs=[pl.BlockSpec((tile, iq_width), lambda i: (i, I_Q_OFF // iq_width)),
                  pl.BlockSpec((tile, w_idx.shape[1]), lambda i: (i, 0)),
                  _resident((s, IDX_DIM), lambda i: (0, 0))],
        out_specs=pl.BlockSpec((1, n_tiles, tile, tile), lambda i: (i, 0, 0, 0)),
        out_shape=jax.ShapeDtypeStruct((n_tiles, n_tiles, tile, tile), BF16),
        scratch_shapes=[pltpu.VMEM((n_tiles, tile, tile), jnp.int32),
                        pltpu.VMEM((IDX_HEADS, tile, LANES), F32),
                        pltpu.VMEM((tile, 1), jnp.int32)],
        compiler_params=_params(1),
        name="dsa_select",
    )(h16, w_idx, k_idx)


def _dsa_attn_kernel(bias_ref, mask_ref, q_ref, k_ref, v_ref, o_ref, m_ref, l_ref, acc_ref, *, tile):
    i = pl.program_id(0)
    q = _scaled_q(q_ref)
    m_ref[...] = jnp.full(m_ref.shape, M_INIT, F32)
    l_ref[...] = jnp.zeros(l_ref.shape, F32)
    acc_ref[...] = jnp.zeros(acc_ref.shape, F32)

    def step(j, near):
        row0 = pl.multiple_of(j * tile, tile)
        kj = k_ref[pl.ds(row0, tile), :]
        vj = v_ref[pl.ds(row0, tile), :]
        mask = mask_ref[0, j].astype(F32)
        for h in range(C_HEADS):
            cols = slice(h * HEAD_DIM, (h + 1) * HEAD_DIM)
            s = lax.dot_general(q[:, cols], kj[:, cols], NT_DIMS, preferred_element_type=F32) + mask
            if near is not None:
                s = s + bias_ref[h, near]
            _softmax_step(s, vj[:, cols], m_ref, l_ref, acc_ref, h, (slice(None), cols))

    def far_body(j, carry):
        step(j, None)
        return carry

    lax.fori_loop(0, jnp.maximum(i - 1, 0), far_body, 0)

    @pl.when(i >= 1)
    def _():
        step(i - 1, 1)

    step(i, 0)
    for h in range(C_HEADS):
        cols = slice(h * HEAD_DIM, (h + 1) * HEAD_DIM)
        o_ref[:, cols] = acc_ref[:, cols] / l_ref[h]


def _dsa_attention(h16, mask, bias, *, tile):
    s = h16.shape[0]
    n_tiles = s // tile
    return pl.pallas_call(
        functools.partial(_dsa_attn_kernel, tile=tile),
        grid=(n_tiles,),
        in_specs=[_resident(bias.shape, lambda i: (0, 0, 0, 0)),
                  pl.BlockSpec((1, n_tiles, tile, tile), lambda i: (i, 0, 0, 0)),
                  pl.BlockSpec((tile, C_WIDTH), lambda i: (i, C_Q_OFF // C_WIDTH)),
                  _resident((s, C_WIDTH), lambda i: (0, C_K_OFF // C_WIDTH)),
                  _resident((s, C_WIDTH), lambda i: (0, C_V_OFF // C_WIDTH))],
        out_specs=pl.BlockSpec((tile, C_WIDTH), lambda i: (i, 0)),
        out_shape=jax.ShapeDtypeStruct((s, C_WIDTH), F32),
        scratch_shapes=[pltpu.VMEM((C_HEADS, tile, 1), F32), pltpu.VMEM((C_HEADS, tile, 1), F32),
                        pltpu.VMEM((tile, C_WIDTH), F32)],
        compiler_params=_params(1),
        name="dsa_attention",
    )(bias, mask, h16, h16, h16)


def kernel(x, w_in, w_out, lam_vecs, a_norm_g, b_norm_g, c_norm_g, idx_k_norm_g, idx_k_norm_b,
           ln1_g, ln1_b, w_up, w_down, ln2_g, ln2_b, rel_bias):
    batch, seq, d_model = x.shape
    depth = w_in.shape[0]
    assert batch == 1 and d_model == D_MODEL and w_in.shape[2] == N_IN
    assert seq % ATT_TILE == 0
    alpha = (2 * depth) ** 0.25
    top_k = min(TOPK_CAP, seq // 4)
    tile = ATT_TILE
    tm = min(512, seq)

    bias_a = _bias_tiles(rel_bias[:, :A_HEADS], tile)
    bias_c = _bias_tiles(rel_bias[:, A_HEADS:], tile)

    x32 = x[0]
    x16 = x32.astype(BF16)
    for l in range(depth):
        lam_init = 0.8 - 0.6 * math.exp(-0.3 * l)
        w_main = w_in[l, :, :I_K_OFF].astype(BF16)
        w_tail = jnp.pad(w_in[l, :, I_K_OFF:], ((0, 0), (0, TAIL_WIDTH - (N_IN - I_K_OFF)))).astype(BF16)

        h16 = _proj(x16, w_main, tm=tm, tn=1024)
        k_idx, w_idx = _tail_proj(x16, w_tail, idx_k_norm_g[l][None], idx_k_norm_b[l][None], tm=tm)

        a16 = _diff_attention(h16, lam_vecs[l], a_norm_g[l][None], bias_a, lam_init=lam_init, tile=tile)
        b_raw = _stick_breaking_attention(h16, tile=tile)
        mask = _dsa_select(h16, w_idx, k_idx, tile=tile, top_k=top_k)
        c_raw = _dsa_attention(h16, mask, bias_c, tile=tile)

        x32, x16 = _out_proj_ln(a16, b_raw, c_raw, b_norm_g[l][None], c_norm_g[l][None],
                                w_out[l].astype(BF16), x32, ln1_g[l][None], ln1_b[l][None],
                                alpha=alpha, tm=tm)
        hdn = _proj(x16, w_up[l].astype(BF16), tm=tm, tn=1024, relu2=True)
        x32, x16 = _down_proj_ln(hdn, w_down[l].astype(BF16), x32, ln2_g[l][None], ln2_b[l][None],
                                 alpha=alpha, tm=tm, tk=1024)
    return x32[None]
```

```python
import functools
import math

import jax
import jax.numpy as jnp
from jax import lax
from jax.experimental import pallas as pl
from jax.experimental.pallas import tpu as pltpu

F32 = jnp.float32
BF16 = jnp.bfloat16

D_MODEL = 2048
HEAD_DIM = 128
A_HEADS = 4
A_VDIM = 2 * HEAD_DIM
B_HEADS = 4
C_HEADS = 4
IDX_HEADS = 16
IDX_DIM = 128
D_FF = 4 * D_MODEL
N_BUCKETS = 32
MAX_DISTANCE = 128
TOPK_CAP = 256
LN_EPS = 1e-5
RMS_EPS = 1e-5
NEG_INF = -1e30

A_Q_OFF = 0
A_K_OFF = A_Q_OFF + A_HEADS * 2 * HEAD_DIM
A_V_OFF = A_K_OFF + A_HEADS * 2 * HEAD_DIM
B_Q_OFF = A_V_OFF + A_HEADS * A_VDIM
B_K_OFF = B_Q_OFF + B_HEADS * HEAD_DIM
B_V_OFF = B_K_OFF + B_HEADS * HEAD_DIM
C_Q_OFF = B_V_OFF + B_HEADS * HEAD_DIM
C_K_OFF = C_Q_OFF + C_HEADS * HEAD_DIM
C_V_OFF = C_K_OFF + C_HEADS * HEAD_DIM
I_Q_OFF = C_V_OFF + C_HEADS * HEAD_DIM
I_K_OFF = I_Q_OFF + IDX_HEADS * IDX_DIM
I_W_OFF = I_K_OFF + IDX_DIM
N_IN = I_W_OFF + IDX_HEADS
A_WIDTH = A_HEADS * A_VDIM
B_WIDTH = B_HEADS * HEAD_DIM
C_WIDTH = C_HEADS * HEAD_DIM

LANES = 128
TILE_A = 512
TILE_B = 256
TILE_C = 512
EXP_ZERO = -105.0
TAIL_WIDTH = 2 * LANES
VMEM_LIMIT = 56 * 1024 * 1024

MASK_NEG = -1e30
M_INIT = -1e29
INT_MIN = -(2 ** 31)

NT_DIMS = (((1,), (1,)), ((), ()))


def _params(n_axes):
    return pltpu.CompilerParams(dimension_semantics=("arbitrary",) * n_axes,
                                vmem_limit_bytes=VMEM_LIMIT)


def _resident(block_shape, index_map):
    return pl.BlockSpec(block_shape, index_map, pipeline_mode=pl.Buffered(1))


def _layer_norm(y, g, b):
    mu = jnp.mean(y, axis=-1, keepdims=True)
    d = y - mu
    var = jnp.mean(d * d, axis=-1, keepdims=True)
    return d * lax.rsqrt(var + LN_EPS) * g + b


def _rms_norm(y, g):
    return y * lax.rsqrt(jnp.mean(y * y, axis=-1, keepdims=True) + RMS_EPS) * g


def _proj_kernel(x_ref, w_ref, o_ref, *, relu2):
    y = jnp.dot(x_ref[...], w_ref[...], preferred_element_type=F32)
    if relu2:
        y = jnp.square(jnp.maximum(y, 0.0))
    o_ref[...] = y.astype(o_ref.dtype)


def _proj(x16, w16, *, tm, tn, relu2=False):
    m, k = x16.shape
    n = w16.shape[1]
    return pl.pallas_call(
        functools.partial(_proj_kernel, relu2=relu2),
        grid=(n // tn, m // tm),
        in_specs=[pl.BlockSpec((tm, k), lambda j, i: (i, 0)),
                  pl.BlockSpec((k, tn), lambda j, i: (0, j))],
        out_specs=pl.BlockSpec((tm, tn), lambda j, i: (i, j)),
        out_shape=jax.ShapeDtypeStruct((m, n), BF16),
        compiler_params=_params(2),
        name="proj_relu2" if relu2 else "proj",
    )(x16, w16)


def _tail_kernel(x_ref, w_ref, g_ref, b_ref, kidx_ref, widx_ref):
    y = jnp.dot(x_ref[...], w_ref[...], preferred_element_type=F32)
    kidx_ref[...] = _layer_norm(y[:, :IDX_DIM], g_ref[...], b_ref[...]).astype(BF16)
    widx_ref[...] = y[:, IDX_DIM:] * (IDX_HEADS ** -0.5 * IDX_DIM ** -0.5)


def _tail_proj(x16, w_tail16, g, b, *, tm):
    m, k = x16.shape
    return pl.pallas_call(
        _tail_kernel,
        grid=(m // tm,),
        in_specs=[pl.BlockSpec((tm, k), lambda i: (i, 0)),
                  _resident((k, TAIL_WIDTH), lambda i: (0, 0)),
                  _resident((1, IDX_DIM), lambda i: (0, 0)),
                  _resident((1, IDX_DIM), lambda i: (0, 0))],
        out_specs=[pl.BlockSpec((tm, IDX_DIM), lambda i: (i, 0)),
                   pl.BlockSpec((tm, TAIL_WIDTH - IDX_DIM), lambda i: (i, 0))],
        out_shape=[jax.ShapeDtypeStruct((m, IDX_DIM), BF16),
                   jax.ShapeDtypeStruct((m, TAIL_WIDTH - IDX_DIM), F32)],
        compiler_params=_params(1),
        name="tail_proj",
    )(x16, w_tail16, g, b)


def _residual_ln(x, acc, alpha, g, b, o32_ref, o16_ref):
    y = _layer_norm(alpha * x + acc, g, b)
    o32_ref[...] = y
    o16_ref[...] = y.astype(BF16)


def _out_ln_kernel(a_ref, b_ref, c_ref, gb_ref, gc_ref, w_ref, x_ref, lg_ref, lb_ref,
                   o32_ref, o16_ref, *, alpha):
    bn = _rms_norm(b_ref[...], gb_ref[...]).astype(BF16)
    cn = _rms_norm(c_ref[...], gc_ref[...]).astype(BF16)
    acc = jnp.dot(a_ref[...], w_ref[:A_WIDTH, :], preferred_element_type=F32)
    acc += jnp.dot(bn, w_ref[A_WIDTH:A_WIDTH + B_WIDTH, :], preferred_element_type=F32)
    acc += jnp.dot(cn, w_ref[A_WIDTH + B_WIDTH:, :], preferred_element_type=F32)
    _residual_ln(x_ref[...], acc, alpha, lg_ref[...], lb_ref[...], o32_ref, o16_ref)


def _out_proj_ln(a16, b_raw, c_raw, gb, gc, w16, x32, lg, lb, *, alpha, tm):
    m, d = x32.shape
    row = lambda i: (i, 0)
    fixed = lambda i: (0, 0)
    return pl.pallas_call(
        functools.partial(_out_ln_kernel, alpha=alpha),
        grid=(m // tm,),
        in_specs=[pl.BlockSpec((tm, A_WIDTH), row),
                  pl.BlockSpec((tm, B_WIDTH), row),
                  pl.BlockSpec((tm, C_WIDTH), row),
                  _resident((1, B_WIDTH), fixed),
                  _resident((1, C_WIDTH), fixed),
                  _resident(w16.shape, fixed),
                  pl.BlockSpec((tm, d), row),
                  _resident((1, d), fixed),
                  _resident((1, d), fixed)],
        out_specs=[pl.BlockSpec((tm, d), row), pl.BlockSpec((tm, d), row)],
        out_shape=[jax.ShapeDtypeStruct((m, d), F32), jax.ShapeDtypeStruct((m, d), BF16)],
        compiler_params=_params(1),
        name="out_proj_ln",
    )(a16, b_raw, c_raw, gb, gc, w16, x32, lg, lb)


def _down_ln_kernel(h_ref, w_ref, x_ref, lg_ref, lb_ref, o32_ref, o16_ref, acc_ref, *, alpha):
    kk = pl.program_id(1)

    @pl.when(kk == 0)
    def _():
        acc_ref[...] = jnp.zeros(acc_ref.shape, F32)

    acc_ref[...] += jnp.dot(h_ref[...], w_ref[...], preferred_element_type=F32)

    @pl.when(kk == pl.num_programs(1) - 1)
    def _():
        _residual_ln(x_ref[...], acc_ref[...], alpha, lg_ref[...], lb_ref[...], o32_ref, o16_ref)


def _down_proj_ln(h16, w16, x32, lg, lb, *, alpha, tm, tk):
    m, d = x32.shape
    k = h16.shape[1]
    return pl.pallas_call(
        functools.partial(_down_ln_kernel, alpha=alpha),
        grid=(m // tm, k // tk),
        in_specs=[pl.BlockSpec((tm, tk), lambda i, kk: (i, kk)),
                  pl.BlockSpec((tk, d), lambda i, kk: (kk, 0)),
                  pl.BlockSpec((tm, d), lambda i, kk: (i, 0)),
                  _resident((1, d), lambda i, kk: (0, 0)),
                  _resident((1, d), lambda i, kk: (0, 0))],
        out_specs=[pl.BlockSpec((tm, d), lambda i, kk: (i, 0)),
                   pl.BlockSpec((tm, d), lambda i, kk: (i, 0))],
        out_shape=[jax.ShapeDtypeStruct((m, d), F32), jax.ShapeDtypeStruct((m, d), BF16)],
        scratch_shapes=[pltpu.VMEM((tm, d), F32)],
        compiler_params=_params(2),
        name="down_proj_ln",
    )(h16, w16, x32, lg, lb)


def _rel_bucket(dist):
    n = jnp.maximum(dist, 0)
    max_exact = N_BUCKETS // 2
    nf = jnp.maximum(n, 1).astype(F32)
    large = max_exact + (jnp.log(nf / max_exact) / math.log(MAX_DISTANCE / max_exact)
                         * (N_BUCKETS - max_exact)).astype(jnp.int32)
    large = jnp.minimum(large, N_BUCKETS - 1)
    return jnp.where(n < max_exact, n, large)


def _bias_tiles(table, tile):
    assert tile >= MAX_DISTANCE
    table = table.astype(F32)
    qi = jnp.arange(tile)[:, None]
    kj = jnp.arange(tile)[None, :]
    far = table[_rel_bucket(jnp.asarray(2 * tile))]
    d_diag = qi - kj
    diag = jnp.where((d_diag >= 0)[..., None], table[_rel_bucket(d_diag)] - far, NEG_INF)
    prev = table[_rel_bucket(tile + d_diag)] - far
    return jnp.transpose(jnp.stack([diag, prev], axis=0), (3, 0, 1, 2))


def _scaled_q(q_ref):
    return (q_ref[...].astype(F32) * (HEAD_DIM ** -0.5)).astype(BF16)


def _softmax_step(s, v, m_ref, l_ref, acc_ref, mi, acc_cols):
    m_old = m_ref[mi]
    m_new = jnp.maximum(m_old, jnp.max(s, axis=-1, keepdims=True))
    p = jnp.exp(s - m_new)
    alpha = jnp.exp(m_old - m_new)
    l_ref[mi] = alpha * l_ref[mi] + jnp.sum(p, axis=-1, keepdims=True)
    acc_ref[acc_cols] = alpha * acc_ref[acc_cols] + jnp.dot(p.astype(BF16), v,
                                                           preferred_element_type=F32)
    m_ref[mi] = m_new


def _diff_attn_kernel(lamv_ref, g_ref, bias_ref, q_ref, k_ref, v_ref, o_ref,
                      m_ref, l_ref, acc_ref, *, tile, lam_init):
    i = pl.program_id(1)
    q = _scaled_q(q_ref)
    m_ref[...] = jnp.full(m_ref.shape, M_INIT, F32)
    l_ref[...] = jnp.zeros(l_ref.shape, F32)
    acc_ref[...] = jnp.zeros(acc_ref.shape, F32)

    def step(j, bias):
        row0 = pl.multiple_of(j * tile, tile)
        kj = k_ref[pl.ds(row0, tile), :]
        vj = v_ref[pl.ds(row0, tile), :]
        for mm in range(2):
            cols = slice(mm * HEAD_DIM, (mm + 1) * HEAD_DIM)
            s = lax.dot_general(q[:, cols], kj[:, cols], NT_DIMS, preferred_element_type=F32)
            if bias is not None:
                s = s + bias
            _softmax_step(s, vj, m_ref, l_ref, acc_ref, mm, mm)

    def far_body(j, carry):
        step(j, None)
        return carry

    lax.fori_loop(0, jnp.maximum(i - 1, 0), far_body, 0)

    @pl.when(i >= 1)
    def _():
        step(i - 1, bias_ref[0, 1])

    step(i, bias_ref[0, 0])

    lv = lamv_ref[...]
    lam = (jnp.exp(jnp.sum(lv[0:1] * lv[1:2], axis=-1, keepdims=True))
           - jnp.exp(jnp.sum(lv[2:3] * lv[3:4], axis=-1, keepdims=True)) + lam_init)
    o = acc_ref[0] / l_ref[0] - lam * (acc_ref[1] / l_ref[1])
    o_ref[...] = (_rms_norm(o, g_ref[...]) * (1.0 - lam_init)).astype(o_ref.dtype)


def _diff_attention(h16, lam_vecs, g, bias, *, lam_init, tile):
    s = h16.shape[0]
    qb, kb, vb = A_Q_OFF // A_VDIM, A_K_OFF // A_VDIM, A_V_OFF // A_VDIM
    return pl.pallas_call(
        functools.partial(_diff_attn_kernel, tile=tile, lam_init=lam_init),
        grid=(A_HEADS, s // tile),
        in_specs=[_resident(lam_vecs.shape, lambda h, i: (0, 0)),
                  _resident((1, A_VDIM), lambda h, i: (0, 0)),
                  _resident((1, 2, tile, tile), lambda h, i: (h, 0, 0, 0)),
                  pl.BlockSpec((tile, 2 * HEAD_DIM), lambda h, i: (i, qb + h)),
                  _resident((s, 2 * HEAD_DIM), lambda h, i: (0, kb + h)),
                  _resident((s, A_VDIM), lambda h, i: (0, vb + h))],
        out_specs=pl.BlockSpec((tile, A_VDIM), lambda h, i: (i, h)),
        out_shape=jax.ShapeDtypeStruct((s, A_WIDTH), BF16),
        scratch_shapes=[pltpu.VMEM((2, tile, 1), F32), pltpu.VMEM((2, tile, 1), F32),
                        pltpu.VMEM((2, tile, A_VDIM), F32)],
        compiler_params=_params(2),
        name="diff_attention",
    )(lam_vecs, g, bias, h16, h16, h16)


def _stick_kernel(u_ref, q_ref, k_ref, v_ref, o_ref, c_ref, acc_ref, *, tile):
    i = pl.program_id(1)
    q = _scaled_q(q_ref)
    c_ref[...] = jnp.zeros(c_ref.shape, F32)
    acc_ref[...] = jnp.zeros(acc_ref.shape, F32)

    def step(j, diag):
        row0 = pl.multiple_of(j * tile, tile)
        kj = k_ref[pl.ds(row0, tile), :]
        vj = v_ref[pl.ds(row0, tile), :]
        z = lax.dot_general(q, kj, NT_DIMS, preferred_element_type=F32)
        sp = jnp.log1p(jnp.exp(-jnp.abs(z)))
        log_beta = jnp.minimum(z, 0.0) - sp
        log_keep = -(jnp.maximum(z, 0.0) + sp)
        if diag:
            strict = (lax.broadcasted_iota(jnp.int32, z.shape, 0)
                      > lax.broadcasted_iota(jnp.int32, z.shape, 1))
            log_keep = jnp.where(strict, log_keep, 0.0)
        hi = log_keep.astype(BF16)
        lo = (log_keep - hi.astype(F32)).astype(BF16)
        u = u_ref[...]
        cum = (jnp.dot(hi, u, preferred_element_type=F32)
               + jnp.dot(lo, u, preferred_element_type=F32))
        a = jnp.exp(log_beta + cum + c_ref[...])
        if diag:
            a = jnp.where(strict, a, 0.0)
        acc_ref[...] += jnp.dot(a.astype(BF16), vj, preferred_element_type=F32)
        c_ref[...] += cum[:, 0:1] + log_keep[:, 0:1]

    step(i, True)

    def alive():
        return jnp.max(c_ref[...]) > EXP_ZERO

    def cond(carry):
        t, live = carry
        return jnp.logical_and(t < i, live)

    def body(carry):
        t, _ = carry
        step(i - 1 - t, False)
        return t + 1, alive()

    lax.while_loop(cond, body, (jnp.int32(0), alive()))
    o_ref[...] = acc_ref[...]


def _stick_breaking_attention(h16, *, tile):
    s = h16.shape[0]
    qb, kb, vb = B_Q_OFF // HEAD_DIM, B_K_OFF // HEAD_DIM, B_V_OFF // HEAD_DIM
    idx = jnp.arange(tile)
    upper = (idx[:, None] > idx[None, :]).astype(BF16)
    return pl.pallas_call(
        functools.partial(_stick_kernel, tile=tile),
        grid=(B_HEADS, s // tile),
        in_specs=[_resident((tile, tile), lambda h, i: (0, 0)),
                  pl.BlockSpec((tile, HEAD_DIM), lambda h, i: (i, qb + h)),
                  _resident((s, HEAD_DIM), lambda h, i: (0, kb + h)),
                  _resident((s, HEAD_DIM), lambda h, i: (0, vb + h))],
        out_specs=pl.BlockSpec((tile, HEAD_DIM), lambda h, i: (i, h)),
        out_shape=jax.ShapeDtypeStruct((s, B_WIDTH), F32),
        scratch_shapes=[pltpu.VMEM((tile, 1), F32), pltpu.VMEM((tile, HEAD_DIM), F32)],
        compiler_params=_params(2),
        name="stick_breaking_attention",
    )(upper, h16, h16, h16)


def _dsa_select_kernel(iq_ref, w_ref, kidx_ref, o_ref, key_ref, wb_ref, bound_ref,
                       *, tile, top_k, seq):
    i = pl.program_id(0)
    n_tiles = seq // tile
    kf = float(top_k)
    row = lax.broadcasted_iota(jnp.int32, (tile, tile), 0)
    col = lax.broadcasted_iota(jnp.int32, (tile, tile), 1)
    causal = row >= col

    w = w_ref[...]
    for h in range(IDX_HEADS):
        wb_ref[h] = jnp.broadcast_to(w[:, h:h + 1], (tile, LANES))

    def score_tile(c, carry):
        row0 = pl.multiple_of(c * tile, tile)
        kc = kidx_ref[pl.ds(row0, tile), :]
        acc = jnp.zeros((tile, tile), F32)
        for h in range(IDX_HEADS):
            lg = lax.dot_general(iq_ref[:, h * IDX_DIM:(h + 1) * IDX_DIM], kc, NT_DIMS,
                                 preferred_element_type=F32)
            wb = wb_ref[h]
            acc = acc + jnp.maximum(lg, 0.0) * jnp.concatenate([wb] * (tile // LANES), axis=1)
        bits = lax.bitcast_convert_type(acc, jnp.int32)
        key = bits ^ ((bits >> 31) & 0x7FFFFFFF)
        key_ref[c] = jnp.where(bits == INT_MIN, 0, key)
        return carry

    lax.fori_loop(0, i + 1, score_tile, 0)
    key_ref[i] = jnp.where(causal, key_ref[i], INT_MIN)

    def fold(x):
        out = x[:, :LANES]
        for t in range(1, tile // LANES):
            out = out + x[:, t * LANES:(t + 1) * LANES]
        return out

    def count(pred):
        def body(c, cnt):
            return cnt + fold(jnp.where(pred(key_ref[c], c), 1.0, 0.0))
        cnt = lax.fori_loop(0, i + 1, body, jnp.zeros((tile, LANES), F32))
        return jnp.sum(cnt, axis=-1, keepdims=True)

    zero = jnp.zeros((tile, 1), jnp.int32)
    thr = jnp.where(count(lambda k, c: k >= zero) >= kf, zero, jnp.full((tile, 1), INT_MIN, jnp.int32))

    def bit_body(t, thr):
        cand = thr + jnp.left_shift(jnp.int32(1), 30 - t)
        return jnp.where(count(lambda k, c: k >= cand) >= kf, cand, thr)

    thr = lax.fori_loop(0, 31, bit_body, thr)

    n_gt = count(lambda k, c: k > thr)
    n_eq = count(lambda k, c: k == thr)
    need = kf - n_gt
    bound_ref[...] = jnp.full((tile, 1), seq, jnp.int32)

    @pl.when(jnp.max(jnp.where((n_eq > need) & (thr > INT_MIN), 1.0, 0.0)) > 0.0)
    def _():
        def idx_body(t, pos):
            cand = pos + jnp.left_shift(jnp.int32(1), (seq.bit_length() - 1) - t)
            n = count(lambda k, c: (k == thr) & ((c * tile + col) < cand))
            return jnp.where(n < need, cand, pos)
        pos = lax.fori_loop(0, seq.bit_length(), idx_body, jnp.zeros((tile, 1), jnp.int32))
        bound_ref[...] = pos + 1

    bound = bound_ref[...]

    def selected(c):
        k = key_ref[c]
        return (k > thr) | ((k == thr) & ((c * tile + col) < bound))

    def emit(c, carry):
        o_ref[0, c] = jnp.where(selected(c), 0.0, MASK_NEG).astype(o_ref.dtype)
        return carry

    lax.fori_loop(0, i, emit, 0)
    o_ref[0, i] = jnp.where(selected(i) & causal, 0.0, MASK_NEG).astype(o_ref.dtype)

    def fill(c, carry):
        o_ref[0, c] = jnp.full((tile, tile), MASK_NEG, o_ref.dtype)
        return carry

    lax.fori_loop(i + 1, n_tiles, fill, 0)


def _dsa_select(h16, w_idx, k_idx, *, tile, top_k):
    s = h16.shape[0]
    n_tiles = s // tile
    iq_width = IDX_HEADS * IDX_DIM
    return pl.pallas_call(
        functools.partial(_dsa_select_kernel, tile=tile, top_k=top_k, seq=s),
        grid=(n_tiles,),
        in_specs=[pl.BlockSpec((tile, iq_width), lambda i: (i, I_Q_OFF // iq_width)),
                  pl.BlockSpec((tile, w_idx.shape[1]), lambda i: (i, 0)),
                  _resident((s, IDX_DIM), lambda i: (0, 0))],
        out_specs=pl.BlockSpec((1, n_tiles, tile, tile), lambda i: (i, 0, 0, 0)),
        out_shape=jax.ShapeDtypeStruct((n_tiles, n_tiles, tile, tile), BF16),
        scratch_shapes=[pltpu.VMEM((n_tiles, tile, tile), jnp.int32),
                        pltpu.VMEM((IDX_HEADS, tile, LANES), F32),
                        pltpu.VMEM((tile, 1), jnp.int32)],
        compiler_params=_params(1),
        name="dsa_select",
    )(h16, w_idx, k_idx)


def _dsa_attn_kernel(bias_ref, mask_ref, q_ref, k_ref, v_ref, o_ref, m_ref, l_ref, acc_ref, *, tile):
    i = pl.program_id(0)
    q = _scaled_q(q_ref)
    m_ref[...] = jnp.full(m_ref.shape, M_INIT, F32)
    l_ref[...] = jnp.zeros(l_ref.shape, F32)
    acc_ref[...] = jnp.zeros(acc_ref.shape, F32)

    def step(j, near):
        row0 = pl.multiple_of(j * tile, tile)
        kj = k_ref[pl.ds(row0, tile), :]
        vj = v_ref[pl.ds(row0, tile), :]
        mask = mask_ref[0, j].astype(F32)
        for h in range(C_HEADS):
            cols = slice(h * HEAD_DIM, (h + 1) * HEAD_DIM)
            s = lax.dot_general(q[:, cols], kj[:, cols], NT_DIMS, preferred_element_type=F32) + mask
            if near is not None:
                s = s + bias_ref[h, near]
            _softmax_step(s, vj[:, cols], m_ref, l_ref, acc_ref, h, (slice(None), cols))

    def far_body(j, carry):
        step(j, None)
        return carry

    lax.fori_loop(0, jnp.maximum(i - 1, 0), far_body, 0)

    @pl.when(i >= 1)
    def _():
        step(i - 1, 1)

    step(i, 0)
    for h in range(C_HEADS):
        cols = slice(h * HEAD_DIM, (h + 1) * HEAD_DIM)
        o_ref[:, cols] = acc_ref[:, cols] / l_ref[h]


def _dsa_attention(h16, mask, bias, *, tile):
    s = h16.shape[0]
    n_tiles = s // tile
    return pl.pallas_call(
        functools.partial(_dsa_attn_kernel, tile=tile),
        grid=(n_tiles,),
        in_specs=[_resident(bias.shape, lambda i: (0, 0, 0, 0)),
                  pl.BlockSpec((1, n_tiles, tile, tile), lambda i: (i, 0, 0, 0)),
                  pl.BlockSpec((tile, C_WIDTH), lambda i: (i, C_Q_OFF // C_WIDTH)),
                  _resident((s, C_WIDTH), lambda i: (0, C_K_OFF // C_WIDTH)),
                  _resident((s, C_WIDTH), lambda i: (0, C_V_OFF // C_WIDTH))],
        out_specs=pl.BlockSpec((tile, C_WIDTH), lambda i: (i, 0)),
        out_shape=jax.ShapeDtypeStruct((s, C_WIDTH), F32),
        scratch_shapes=[pltpu.VMEM((C_HEADS, tile, 1), F32), pltpu.VMEM((C_HEADS, tile, 1), F32),
                        pltpu.VMEM((tile, C_WIDTH), F32)],
        compiler_params=_params(1),
        name="dsa_attention",
    )(bias, mask, h16, h16, h16)


def kernel(x, w_in, w_out, lam_vecs, a_norm_g, b_norm_g, c_norm_g, idx_k_norm_g, idx_k_norm_b,
           ln1_g, ln1_b, w_up, w_down, ln2_g, ln2_b, rel_bias):
    batch, seq, d_model = x.shape
    depth = w_in.shape[0]
    assert batch == 1 and d_model == D_MODEL and w_in.shape[2] == N_IN
    tile_a, tile_b, tile_c = min(TILE_A, seq), min(TILE_B, seq), min(TILE_C, seq)
    assert seq % tile_a == 0 and seq % tile_b == 0 and seq % tile_c == 0
    alpha = (2 * depth) ** 0.25
    top_k = min(TOPK_CAP, seq // 4)
    tm = min(512, seq)

    bias_a = _bias_tiles(rel_bias[:, :A_HEADS], tile_a)
    bias_c = _bias_tiles(rel_bias[:, A_HEADS:], tile_c)

    x32 = x[0]
    x16 = x32.astype(BF16)
    for l in range(depth):
        lam_init = 0.8 - 0.6 * math.exp(-0.3 * l)
        w_main = w_in[l, :, :I_K_OFF].astype(BF16)
        w_tail = jnp.pad(w_in[l, :, I_K_OFF:], ((0, 0), (0, TAIL_WIDTH - (N_IN - I_K_OFF)))).astype(BF16)

        h16 = _proj(x16, w_main, tm=tm, tn=1024)
        k_idx, w_idx = _tail_proj(x16, w_tail, idx_k_norm_g[l][None], idx_k_norm_b[l][None], tm=tm)

        a16 = _diff_attention(h16, lam_vecs[l], a_norm_g[l][None], bias_a, lam_init=lam_init, tile=tile_a)
        b_raw = _stick_breaking_attention(h16, tile=tile_b)
        mask = _dsa_select(h16, w_idx, k_idx, tile=tile_c, top_k=top_k)
        c_raw = _dsa_attention(h16, mask, bias_c, tile=tile_c)

        x32, x16 = _out_proj_ln(a16, b_raw, c_raw, b_norm_g[l][None], c_norm_g[l][None],
                                w_out[l].astype(BF16), x32, ln1_g[l][None], ln1_b[l][None],
                                alpha=alpha, tm=tm)
        hdn = _proj(x16, w_up[l].astype(BF16), tm=tm, tn=1024, relu2=True)
        x32, x16 = _down_proj_ln(hdn, w_down[l].astype(BF16), x32, ln2_g[l][None], ln2_b[l][None],
                                 alpha=alpha, tm=tm, tk=1024)
    return x32[None]
```
